```python
import math
import jax, jax.numpy as jnp
from jax import lax
import numpy as np

D_MODEL = 1024
BATCH = 8
SEQ = 4096
DEPTH = 1

HEAD_DIM = 64
N_FOX_HEADS = 8
N_DIL_HEADS = 8
FOX_WIDTH = N_FOX_HEADS * HEAD_DIM
DIL_WIDTH = N_DIL_HEADS * HEAD_DIM
MIX_WIDTH = FOX_WIDTH + DIL_WIDTH
IN_WIDTH = 3 * FOX_WIDTH + N_FOX_HEADS + 3 * DIL_WIDTH
DILATION_PATTERNS = ((128, 1), (512, 4), (2048, 16))
Q_BLOCK = 128
RMS_EPS = 1e-6
NEG_INF = -1e30

PEER_HEADS = 8
PEER_NKEYS = 128
PEER_EXPERTS = PEER_NKEYS * PEER_NKEYS
PEER_TOPK = 16
PEER_QDIM = 256
PEER_HALF = PEER_QDIM // 2
PEER_CHUNK = 128

kernel_name = "hybrid_fox_dilated_peer_block"


def rms_norm(x, g):
    xf = x.astype(jnp.float32)
    y = xf * lax.rsqrt(jnp.mean(xf * xf, axis=-1, keepdims=True) + RMS_EPS)
    return (y * g.astype(jnp.float32)).astype(x.dtype)


def split_heads(a, n_heads):
    b, t, _ = a.shape
    return a.reshape(b, t, n_heads, HEAD_DIM).transpose(0, 2, 1, 3)


def merge_heads(a):
    b, h, t, d = a.shape
    return a.transpose(0, 2, 1, 3).reshape(b, t, h * d)


def alibi_slopes(n):
    return 2.0 ** (-8.0 * jnp.arange(1, n + 1, dtype=jnp.float32) / n)


def fox_attention(q, k, v, log_f):
    b, h, t, hd = q.shape
    c = jnp.cumsum(log_f, axis=-1)
    scale = 1.0 / math.sqrt(hd)
    kpos = jnp.arange(t)

    def block(i):
        t0 = i * Q_BLOCK
        qb = lax.dynamic_slice_in_dim(q, t0, Q_BLOCK, axis=2)
        cq = lax.dynamic_slice_in_dim(c, t0, Q_BLOCK, axis=2)
        s = jnp.einsum('bhqd,bhkd->bhqk', qb, k).astype(jnp.float32) * scale
        s = s + cq[..., :, None] - c[:, :, None, :]
        qpos = t0 + jnp.arange(Q_BLOCK)
        mask = kpos[None, :] <= qpos[:, None]
        s = jnp.where(mask, s, NEG_INF)
        p = jax.nn.softmax(s, axis=-1)
        return jnp.einsum('bhqk,bhkd->bhqd', p.astype(v.dtype), v)

    out = lax.map(block, jnp.arange(t // Q_BLOCK))
    return out.transpose(1, 2, 0, 3, 4).reshape(b, h, t, hd)


def dilated_window_branch(q, k, v, window, dilation, slopes):
    b, h, t, hd = q.shape
    n_steps = window // dilation
    sub_len = t // dilation
    nb = -(-sub_len // Q_BLOCK)
    pad = nb * Q_BLOCK - sub_len

    def to_sub(a):
        a = a.reshape(b, h, sub_len, dilation, hd).transpose(0, 1, 3, 2, 4)
        a = jnp.pad(a, ((0, 0), (0, 0), (0, 0), (0, pad), (0, 0)))
        return a.reshape(b, h, dilation, nb, Q_BLOCK, hd)

    qs, ks, vs = to_sub(q), to_sub(k), to_sub(v)

    def with_prev(a):
        prev = jnp.concatenate([jnp.zeros_like(a[:, :, :, :1]), a[:, :, :, :-1]], axis=3)
        return jnp.concatenate([prev, a], axis=4)

    kc, vc = with_prev(ks), with_prev(vs)
    s = jnp.einsum('bhrnqd,bhrnkd->bhrnqk', qs, kc).astype(jnp.float32) / math.sqrt(hd)

    qi = jnp.arange(Q_BLOCK)[:, None] + Q_BLOCK
    ki = jnp.arange(2 * Q_BLOCK)[None, :]
    steps = qi - ki
    band = (steps >= 0) & (steps <= n_steps)
    key_sub = jnp.arange(nb)[:, None, None] * Q_BLOCK - Q_BLOCK + ki[None]
    valid = band[None] & (key_sub >= 0)
    bias = -slopes[:, None, None, None, None] * (dilation * steps).astype(jnp.float32)[None, None, None]
    s = jnp.where(valid, s + bias, NEG_INF)
    lse = jax.nn.logsumexp(s, axis=-1)
    p = jnp.exp(s - lse[..., None])
    o = jnp.einsum('bhrnqk,bhrnkd->bhrnqd', p.astype(vc.dtype), vc)

    def from_sub(a, tail):
        a = a.reshape((b, h, dilation, nb * Q_BLOCK) + tail)[:, :, :, :sub_len]
        a = jnp.swapaxes(a, 2, 3)
        return a.reshape((b, h, t) + tail)

    return from_sub(o, (hd,)), from_sub(lse, ())


def dilated_attention(q, k, v):
    slopes = alibi_slopes(q.shape[1])
    outs, lses = [], []
    for window, dilation in DILATION_PATTERNS:
        o, l = dilated_window_branch(q, k, v, window, dilation, slopes)
        outs.append(o)
        lses.append(l)
    w = jax.nn.softmax(jnp.stack(lses, axis=0), axis=0)
    o = jnp.stack(outs, axis=0).astype(jnp.float32)
    return jnp.sum(w[..., None] * o, axis=0).astype(q.dtype)


def peer_ffn(xn, wq, subkeys, u_table, v_table):
    b, t, d = xn.shape
    xt = xn.reshape(-1, PEER_CHUNK, d)

    def chunk(xc):
        q = (xc @ wq).reshape(PEER_CHUNK, PEER_HEADS, 2, PEER_HALF)
        s = jnp.einsum('chpd,hpkd->chpk', q, subkeys).astype(jnp.float32)
        s1, i1 = lax.top_k(s[:, :, 0], PEER_TOPK)
        s2, i2 = lax.top_k(s[:, :, 1], PEER_TOPK)
        cand = (s1[..., :, None] + s2[..., None, :]).reshape(PEER_CHUNK, PEER_HEADS, PEER_TOPK * PEER_TOPK)
        cidx = (i1[..., :, None] * PEER_NKEYS + i2[..., None, :]).reshape(PEER_CHUNK, PEER_HEADS, PEER_TOPK * PEER_TOPK)
        cs, ci = lax.top_k(cand, PEER_TOPK)
        eidx = jnp.take_along_axis(cidx, ci, axis=-1)
        g = jax.nn.softmax(cs, axis=-1)
        ug = jnp.take(u_table, eidx, axis=0)
        vg = jnp.take(v_table, eidx, axis=0)
        hid = jnp.einsum('cd,chkd->chk', xc, ug).astype(jnp.float32)
        a = jax.nn.gelu(hid, approximate=False) * g
        return jnp.einsum('chk,chkd->cd', a.astype(vg.dtype), vg)

    out = lax.map(chunk, xt)
    return out.reshape(b, t, d)


def setup_inputs(seed: int = 0) -> dict:
    key = jax.random.key(seed)
    ks = jax.random.split(key, 14)
    f32 = jnp.float32
    sd = D_MODEL ** -0.5
    x = jax.random.normal(ks[0], (BATCH, SEQ, D_MODEL), f32)
    norm1_g = 1.0 + 0.05 * jax.random.normal(ks[1], (DEPTH, D_MODEL), f32)
    w_in = jnp.concatenate([
        sd * jax.random.normal(ks[2], (DEPTH, D_MODEL, 3 * FOX_WIDTH), f32),
        0.1 * sd * jax.random.normal(ks[3], (DEPTH, D_MODEL, N_FOX_HEADS), f32),
        sd * jax.random.normal(ks[4], (DEPTH, D_MODEL, 3 * DIL_WIDTH), f32),
    ], axis=-1)
    b_forget = 3.0 + 0.1 * jax.random.normal(ks[5], (DEPTH, N_FOX_HEADS), f32)
    w_out = MIX_WIDTH ** -0.5 * jax.random.normal(ks[6], (DEPTH, MIX_WIDTH, D_MODEL), f32)
    norm2_g = 1.0 + 0.05 * jax.random.normal(ks[7], (DEPTH, D_MODEL), f32)
    peer_wq = sd * jax.random.normal(ks[8], (DEPTH, D_MODEL, PEER_HEADS * PEER_QDIM), f32)
    peer_subkeys = PEER_HALF ** -0.5 * jax.random.normal(ks[9], (DEPTH, PEER_HEADS, 2, PEER_NKEYS, PEER_HALF), f32)
    peer_u = sd * jax.random.normal(ks[10], (DEPTH, PEER_EXPERTS, D_MODEL), f32)
    peer_v = (PEER_HEADS ** -0.5) * jax.random.normal(ks[11], (DEPTH, PEER_EXPERTS, D_MODEL), f32)
    normf_g = 1.0 + 0.05 * jax.random.normal(ks[12], (D_MODEL,), f32)
    return {"x": x, "norm1_g": norm1_g, "w_in": w_in, "b_forget": b_forget, "w_out": w_out,
            "norm2_g": norm2_g, "peer_wq": peer_wq, "peer_subkeys": peer_subkeys,
            "peer_u": peer_u, "peer_v": peer_v, "normf_g": normf_g}


def reference(x, norm1_g, w_in, b_forget, w_out, norm2_g, peer_wq, peer_subkeys, peer_u, peer_v, normf_g):
    h = x
    for layer in range(DEPTH):
        xn = rms_norm(h, norm1_g[layer])
        proj = xn @ w_in[layer]
        o0 = 3 * FOX_WIDTH
        o1 = o0 + N_FOX_HEADS
        fq = split_heads(proj[..., :FOX_WIDTH], N_FOX_HEADS)
        fk = split_heads(proj[..., FOX_WIDTH:2 * FOX_WIDTH], N_FOX_HEADS)
        fv = split_heads(proj[..., 2 * FOX_WIDTH:o0], N_FOX_HEADS)
        log_f = jax.nn.log_sigmoid(proj[..., o0:o1].astype(jnp.float32)
                                   + b_forget[layer].astype(jnp.float32)).transpose(0, 2, 1)
        dq = split_heads(proj[..., o1:o1 + DIL_WIDTH], N_DIL_HEADS)
        dk = split_heads(proj[..., o1 + DIL_WIDTH:o1 + 2 * DIL_WIDTH], N_DIL_HEADS)
        dv = split_heads(proj[..., o1 + 2 * DIL_WIDTH:o1 + 3 * DIL_WIDTH], N_DIL_HEADS)
        fox_out = fox_attention(fq, fk, fv, log_f)
        dil_out = dilated_attention(dq, dk, dv)
        mixed = jnp.concatenate([merge_heads(fox_out), merge_heads(dil_out)], axis=-1)
        h = h + mixed @ w_out[layer]
        xn2 = rms_norm(h, norm2_g[layer])
        h = h + peer_ffn(xn2, peer_wq[layer], peer_subkeys[layer], peer_u[layer], peer_v[layer])
    return rms_norm(h, normf_g)
```

```python
import functools
import math

import jax
import jax.numpy as jnp
from jax import lax
from jax.experimental import pallas as pl
from jax.experimental.pallas import tpu as pltpu

F32 = jnp.float32
BF16 = jnp.bfloat16

HEAD_DIM = 64
N_HEADS = 8
WIDTH = N_HEADS * HEAD_DIM
LANES = 128
DILATION_PATTERNS = ((128, 1), (512, 4), (2048, 16))
MAX_WINDOW = 2048
RMS_EPS = 1e-6
NEG = -1e30
LOWEST = -3.0e38

PEER_HEADS = 8
PEER_NKEYS = 128
PEER_TOPK = 16
PEER_HALF = 128

VMEM_LIMIT = 56 * 1024 * 1024

NT_DIMS = (((1,), (1,)), ((), ()))


def _params(*sem):
    return pltpu.CompilerParams(dimension_semantics=sem, vmem_limit_bytes=VMEM_LIMIT)


def _rms(x, g):
    return x * lax.rsqrt(jnp.mean(x * x, axis=-1, keepdims=True) + RMS_EPS) * g


def _in_proj_kernel(x_ref, g_ref, wf_ref, wd_ref, wg_ref, qkvf_ref, qkvd_ref, lg_ref):
    xn = _rms(x_ref[...], g_ref[...]).astype(BF16)
    for c in range(0, 3 * WIDTH, WIDTH):
        qkvf_ref[:, c:c + WIDTH] = jnp.dot(
            xn, wf_ref[:, c:c + WIDTH], preferred_element_type=F32).astype(BF16)
        qkvd_ref[:, c:c + WIDTH] = jnp.dot(
            xn, wd_ref[:, c:c + WIDTH], preferred_element_type=F32).astype(BF16)
    lg_ref[...] = jnp.dot(xn, wg_ref[...], preferred_element_type=F32)


def _in_proj(x2, g, wf, wd, wg, tm):
    n, d = x2.shape
    full = lambda i: (0, 0)
    row = lambda i: (i, 0)
    return pl.pallas_call(
        _in_proj_kernel,
        grid=(n // tm,),
        in_specs=[pl.BlockSpec((tm, d), row), pl.BlockSpec((1, d), full),
                  pl.BlockSpec((d, 3 * WIDTH), full), pl.BlockSpec((d, 3 * WIDTH), full),
                  pl.BlockSpec((d, LANES), full)],
        out_specs=[pl.BlockSpec((tm, 3 * WIDTH), row), pl.BlockSpec((tm, 3 * WIDTH), row),
                   pl.BlockSpec((tm, LANES), row)],
        out_shape=[jax.ShapeDtypeStruct((n, 3 * WIDTH), BF16),
                   jax.ShapeDtypeStruct((n, 3 * WIDTH), BF16),
                   jax.ShapeDtypeStruct((n, LANES), F32)],
        compiler_params=_params("parallel"),
        name="in_proj",
    )(x2, g, wf, wd, wg)


def _forget_cumsum_kernel(lg_ref, b_ref, ccol_ref, crow_ref, *, nblk):
    r = lax.broadcasted_iota(jnp.int32, (LANES, LANES), 0)
    c = lax.broadcasted_iota(jnp.int32, (LANES, LANES), 1)
    tri = (r >= c).astype(F32)

    def body(i, carry):
        t0 = pl.multiple_of(i * LANES, LANES)
        z = lg_ref[0, pl.ds(t0, LANES), :] + b_ref[...]
        log_f = jnp.minimum(z, 0.0) - jnp.log1p(jnp.exp(-jnp.abs(z)))
        cum = jnp.dot(tri, log_f, precision=lax.Precision.HIGHEST,
                      preferred_element_type=F32) + carry
        ccol_ref[0, pl.ds(t0, LANES), :] = cum
        crow_ref[0, :, pl.ds(t0, LANES)] = cum.T[0:N_HEADS, :]
        return cum[LANES - 1:LANES, :]

    lax.fori_loop(0, nblk, body, jnp.zeros((1, LANES), F32))


def _forget_cumsum(lg, bias):
    b, t, _ = lg.shape
    return pl.pallas_call(
        functools.partial(_forget_cumsum_kernel, nblk=t // LANES),
        grid=(b,),
        in_specs=[pl.BlockSpec((1, t, LANES), lambda i: (i, 0, 0)),
                  pl.BlockSpec((1, LANES), lambda i: (0, 0))],
        out_specs=[pl.BlockSpec((1, t, LANES), lambda i: (i, 0, 0)),
                   pl.BlockSpec((1, N_HEADS, t), lambda i: (i, 0, 0))],
        out_shape=[jax.ShapeDtypeStruct((b, t, LANES), F32),
                   jax.ShapeDtypeStruct((b, N_HEADS, t), F32)],
        compiler_params=_params("parallel"),
        name="forget_cumsum",
    )(lg, bias)


def _flash_step(q, k, v, bias, carry):
    m, l, acc = carry
    s = lax.dot_general(q, k, NT_DIMS, preferred_element_type=F32) + bias
    m_new = jnp.maximum(m, jnp.max(s, axis=1, keepdims=True))
    alpha = jnp.exp(m - m_new)
    p = jnp.exp(s - m_new)
    l = alpha * l + jnp.sum(p, axis=1, keepdims=True)
    acc = alpha * acc + jnp.dot(p.astype(BF16), v, preferred_element_type=F32)
    return m_new, l, acc


def _flash_init(tq):
    return (jnp.full((tq, 1), NEG, F32), jnp.zeros((tq, 1), F32),
            jnp.zeros((tq, HEAD_DIM), F32))


def _fox_kernel(q_ref, k_ref, v_ref, ccol_ref, crow_ref, o_ref, *, tq):
    hp = pl.program_id(1)
    i = pl.program_id(2)
    ccol = ccol_ref[0]
    lane = lax.broadcasted_iota(jnp.int32, (tq, LANES), 1)
    row = lax.broadcasted_iota(jnp.int32, (tq, tq), 0)
    col = lax.broadcasted_iota(jnp.int32, (tq, tq), 1)
    outs = []
    for hh in range(2):
        h = hp * 2 + hh
        lo, hi = hh * HEAD_DIM, (hh + 1) * HEAD_DIM
        cq = jnp.sum(jnp.where(lane == h, ccol, 0.0), axis=1, keepdims=True)
        q = q_ref[0, :, lo:hi]

        def step(j, carry, masked, h=h, lo=lo, hi=hi, cq=cq, q=q):
            k0 = pl.multiple_of(j * tq, tq)
            k = k_ref[0, pl.ds(k0, tq), lo:hi]
            v = v_ref[0, pl.ds(k0, tq), lo:hi]
            bias = cq - crow_ref[0, pl.ds(h, 1), pl.ds(k0, tq)]
            if masked:
                bias = jnp.where(col <= row, bias, NEG)
            return _flash_step(q, k, v, bias, carry)

        carry = lax.fori_loop(0, i, lambda j, c: step(j, c, False), _flash_init(tq))
        _, l, acc = step(i, carry, True)
        outs.append(acc / l)
    o_ref[0] = jnp.concatenate(outs, axis=1).astype(BF16)


def _fox_attn(qkv, ccol, crow, tq):
    b, t, _ = qkv.shape
    npair = N_HEADS // 2
    return pl.pallas_call(
        functools.partial(_fox_kernel, tq=tq),
        grid=(b, npair, t // tq),
        in_specs=[pl.BlockSpec((1, tq, LANES), lambda b_, p, i: (b_, i, p)),
                  pl.BlockSpec((1, t, LANES), lambda b_, p, i: (b_, 0, npair + p)),
                  pl.BlockSpec((1, t, LANES), lambda b_, p, i: (b_, 0, 2 * npair + p)),
                  pl.BlockSpec((1, tq, LANES), lambda b_, p, i: (b_, i, 0)),
                  pl.BlockSpec((1, N_HEADS, t), lambda b_, p, i: (b_, 0, 0))],
        out_specs=pl.BlockSpec((1, tq, LANES), lambda b_, p, i: (b_, i, p)),
        out_shape=jax.ShapeDtypeStruct((b, t, WIDTH), BF16),
        compiler_params=_params("parallel", "parallel", "arbitrary"),
        name="fox_attn",
    )(qkv, qkv, qkv, ccol, crow)


def _dil_kernel(q_ref, k_ref, v_ref, bias_ref, o_ref, *, tq, nback):
    i = pl.program_id(2)
    outs = []
    for hh in range(2):
        lo, hi = hh * HEAD_DIM, (hh + 1) * HEAD_DIM
        q = q_ref[0, :, lo:hi]

        def step(d, carry, hh=hh, lo=lo, hi=hi, q=q):
            k0 = pl.multiple_of((i - d) * tq, tq)
            k = k_ref[0, pl.ds(k0, tq), lo:hi]
            v = v_ref[0, pl.ds(k0, tq), lo:hi]
            return _flash_step(q, k, v, bias_ref[hh, d], carry)

        _, l, acc = lax.fori_loop(0, jnp.minimum(i, nback) + 1, step, _flash_init(tq))
        outs.append(acc / l)
    o_ref[0] = jnp.concatenate(outs, axis=1).astype(BF16)


def _dil_bias_table(tq):
    nback = MAX_WINDOW // tq
    d = jnp.arange(nback + 1, dtype=jnp.int32)[:, None, None]
    r = jnp.arange(tq, dtype=jnp.int32)[None, :, None]
    c = jnp.arange(tq, dtype=jnp.int32)[None, None, :]
    dist = d * tq + r - c
    mult = jnp.zeros(dist.shape, F32)
    for window, dil in DILATION_PATTERNS:
        mult = mult + ((dist >= 0) & (dist <= window) & (dist % dil == 0)).astype(F32)
    slopes = 2.0 ** (-8.0 * jnp.arange(1, N_HEADS + 1, dtype=F32) / N_HEADS)
    alibi = -slopes[:, None, None, None] * dist.astype(F32)[None]
    return jnp.where(mult[None] > 0, jnp.log(jnp.maximum(mult, 1.0))[None] + alibi, NEG)


def _dil_attn(qkv, tq):
    b, t, _ = qkv.shape
    npair = N_HEADS // 2
    nback = MAX_WINDOW // tq
    table = _dil_bias_table(tq)
    return pl.pallas_call(
        functools.partial(_dil_kernel, tq=tq, nback=nback),
        grid=(b, npair, t // tq),
        in_specs=[pl.BlockSpec((1, tq, LANES), lambda b_, p, i: (b_, i, p)),
                  pl.BlockSpec((1, t, LANES), lambda b_, p, i: (b_, 0, npair + p)),
                  pl.BlockSpec((1, t, LANES), lambda b_, p, i: (b_, 0, 2 * npair + p)),
                  pl.BlockSpec((2, nback + 1, tq, tq), lambda b_, p, i: (p, 0, 0, 0))],
        out_specs=pl.BlockSpec((1, tq, LANES), lambda b_, p, i: (b_, i, p)),
        out_shape=jax.ShapeDtypeStruct((b, t, WIDTH), BF16),
        compiler_params=_params("parallel", "parallel", "arbitrary"),
        name="dil_attn",
    )(qkv, qkv, qkv, table)


def _out_proj_kernel(fo_ref, do_ref, x_ref, wo_ref, g_ref, h_ref, xn_ref):
    h = (x_ref[...]
         + jnp.dot(fo_ref[...], wo_ref[0:WIDTH, :], preferred_element_type=F32)
         + jnp.dot(do_ref[...], wo_ref[WIDTH:2 * WIDTH, :], preferred_element_type=F32))
    h_ref[...] = h
    xn_ref[...] = _rms(h, g_ref[...]).astype(BF16)


def _out_proj(fo, do, x2, wo, g, tm):
    n, d = x2.shape
    full = lambda i: (0, 0)
    row = lambda i: (i, 0)
    return pl.pallas_call(
        _out_proj_kernel,
        grid=(n // tm,),
        in_specs=[pl.BlockSpec((tm, WIDTH), row), pl.BlockSpec((tm, WIDTH), row),
                  pl.BlockSpec((tm, d), row), pl.BlockSpec((2 * WIDTH, d), full),
                  pl.BlockSpec((1, d), full)],
        out_specs=[pl.BlockSpec((tm, d), row), pl.BlockSpec((tm, d), row)],
        out_shape=[jax.ShapeDtypeStruct((n, d), F32), jax.ShapeDtypeStruct((n, d), BF16)],
        compiler_params=_params("parallel"),
        name="out_proj",
    )(fo, do, x2, wo, g)


def _top_sorted(s):
    vals = []
    for r in range(PEER_TOPK):
        m = jnp.max(s, axis=0, keepdims=True)
        vals.append(m)
        if r + 1 < PEER_TOPK:
            s = jnp.where(s == m, LOWEST, s)
    return jnp.concatenate(vals, axis=0)


def _sum_threshold(v1, v2):
    tm = v1.shape[1]
    rows = lax.broadcasted_iota(jnp.int32, (8, tm), 0)
    cands = [v1[0:1] + v2[0:8], v1[0:1] + v2[8:16], v1[1:2] + v2[0:8]]
    for i in range(2, 8):
        cands.append(jnp.where(rows < PEER_TOPK // (i + 1), v1[i:i + 1] + v2[0:8], LOWEST))
    cands.append(v1[8:16] + v2[0:1])
    c = jnp.concatenate(cands, axis=0)
    top = v1[0:1] + v2[0:1]
    rest = c
    for r in range(PEER_TOPK):
        tau = jnp.max(rest, axis=0, keepdims=True)
        if r + 1 < PEER_TOPK:
            rest = jnp.where(rest == tau, LOWEST, rest)
    z = jnp.sum(jnp.where(c >= tau, jnp.exp(c - top), 0.0), axis=0, keepdims=True)
    return tau, z


def _peer_route_kernel(xn_ref, wq_ref, keys_ref, s1_ref, s2_ref, e1_ref, e2_ref, tau_ref):
    xn = xn_ref[...]

    def head(h, _):
        r0 = pl.multiple_of(h * 2 * PEER_HALF, 2 * PEER_HALF)
        q_t = lax.dot_general(wq_ref[pl.ds(r0, 2 * PEER_HALF), :], xn, NT_DIMS,
                              preferred_element_type=F32).astype(BF16)
        s1 = jnp.dot(keys_ref[2 * h], q_t[0:PEER_HALF], preferred_element_type=F32)
        s2 = jnp.dot(keys_ref[2 * h + 1], q_t[PEER_HALF:], preferred_element_type=F32)
        v1 = _top_sorted(s1)
        v2 = _top_sorted(s2)
        tau, z = _sum_threshold(v1, v2)
        s1_ref[h] = s1
        s2_ref[h] = s2
        e1_ref[h] = jnp.exp(s1 - v1[0:1]) / z
        e2_ref[h] = jnp.exp(s2 - v2[0:1])
        tau_ref[pl.ds(h, 1), :] = tau
        return 0

    lax.fori_loop(0, PEER_HEADS, head, 0)


def _peer_route(xn2, wq_t, keys, tm):
    n, d = xn2.shape
    spec3 = pl.BlockSpec((PEER_HEADS, PEER_NKEYS, tm), lambda i: (0, 0, i))
    shape3 = jax.ShapeDtypeStruct((PEER_HEADS, PEER_NKEYS, n), F32)
    return pl.pallas_call(
        _peer_route_kernel,
        grid=(n // tm,),
        in_specs=[pl.BlockSpec((tm, d), lambda i: (i, 0)),
                  pl.BlockSpec(wq_t.shape, lambda i: (0, 0)),
                  pl.BlockSpec(keys.shape, lambda i: (0, 0, 0))],
        out_specs=[spec3, spec3, spec3, spec3, pl.BlockSpec((PEER_HEADS, tm), lambda i: (0, i))],
        out_shape=[shape3, shape3, shape3, shape3,
                   jax.ShapeDtypeStruct((PEER_HEADS, n), F32)],
        compiler_params=_params("parallel"),
        name="peer_route",
    )(xn2, wq_t, keys)


SUB = 16


def _peer_dense_kernel(xn_ref, u_ref, vt_ref, s1_ref, s2_ref, e1_ref, e2_ref, tau_ref,
                       h_ref, g_ref, o_ref, acc_ref, hid_ref, act_ref, *, te):
    e = pl.program_id(1)

    @pl.when(e == 0)
    def _():
        acc_ref[...] = jnp.zeros_like(acc_ref)

    hid_ref[...] = lax.dot_general(u_ref[...], xn_ref[...], NT_DIMS,
                                   preferred_element_type=F32)
    nsub = PEER_NKEYS // SUB

    def body(idx, _):
        a = idx // nsub
        b0 = pl.multiple_of((idx % nsub) * SUB, SUB)
        r0 = pl.multiple_of(idx * SUB, SUB)
        gate = None
        for h in range(PEER_HEADS):
            s = s1_ref[h, pl.ds(a, 1), :] + s2_ref[h, pl.ds(b0, SUB), :]
            w = e1_ref[h, pl.ds(a, 1), :] * e2_ref[h, pl.ds(b0, SUB), :]
            w = jnp.where(s >= tau_ref[h:h + 1, :], w, 0.0)
            gate = w if gate is None else gate + w
        hid = hid_ref[pl.ds(r0, SUB), :]
        gelu = 0.5 * hid * (1.0 + lax.erf(hid * (1.0 / math.sqrt(2.0))))
        act_ref[pl.ds(r0, SUB), :] = (gelu * gate).astype(BF16)
        return 0

    lax.fori_loop(0, te // SUB, body, 0)
    acc_ref[...] += jnp.dot(vt_ref[...], act_ref[...], preferred_element_type=F32)

    @pl.when(e == pl.num_programs(1) - 1)
    def _():
        o_ref[...] = _rms(h_ref[...] + acc_ref[...].T, g_ref[...])


def _peer_dense(xn2, u, vt, s1, s2, e1, e2, tau, h1, gf, tm, te):
    n, d = xn2.shape
    ne = u.shape[0]
    na = te // PEER_NKEYS
    tok = lambda i, e: (i, 0)
    a_spec = pl.BlockSpec((PEER_HEADS, na, tm), lambda i, e: (0, e, i))
    b_spec = pl.BlockSpec((PEER_HEADS, PEER_NKEYS, tm), lambda i, e: (0, 0, i))
    return pl.pallas_call(
        functools.partial(_peer_dense_kernel, te=te),
        grid=(n // tm, ne // te),
        in_specs=[pl.BlockSpec((tm, d), tok),
                  pl.BlockSpec((te, d), lambda i, e: (e, 0)),
                  pl.BlockSpec((d, te), lambda i, e: (0, e)),
                  a_spec, b_spec, a_spec, b_spec,
                  pl.BlockSpec((PEER_HEADS, tm), lambda i, e: (0, i)),
                  pl.BlockSpec((tm, d), tok),
                  pl.BlockSpec((1, d), lambda i, e: (0, 0))],
        out_specs=pl.BlockSpec((tm, d), tok),
        out_shape=jax.ShapeDtypeStruct((n, d), F32),
        scratch_shapes=[pltpu.VMEM((d, tm), F32), pltpu.VMEM((te, tm), F32),
                        pltpu.VMEM((te, tm), BF16)],
        compiler_params=_params("parallel", "arbitrary"),
        name="peer_dense",
    )(xn2, u, vt, s1, s2, e1, e2, tau, h1, gf)


def kernel(x, norm1_g, w_in, b_forget, w_out, norm2_g, peer_wq, peer_subkeys, peer_u,
           peer_v, normf_g):
    b, t, d = x.shape
    n = b * t
    assert w_in.shape[0] == 1, "single-layer block"
    assert t % 256 == 0 and n % 512 == 0

    w = w_in[0]
    scale = 1.0 / math.sqrt(HEAD_DIM)
    o0, o1 = 3 * WIDTH, 3 * WIDTH + N_HEADS
    wf = jnp.concatenate([w[:, :WIDTH] * scale, w[:, WIDTH:o0]], axis=1).astype(BF16)
    wd = jnp.concatenate([w[:, o1:o1 + WIDTH] * scale, w[:, o1 + WIDTH:]], axis=1).astype(BF16)
    wg = jnp.pad(w[:, o0:o1], ((0, 0), (0, LANES - N_HEADS))).astype(BF16)
    bias = jnp.pad(b_forget[0], (0, LANES - N_HEADS)).reshape(1, LANES).astype(F32)

    x2 = x.reshape(n, d)
    qkvf, qkvd, lg = _in_proj(x2, norm1_g[0].reshape(1, d), wf, wd, wg, tm=512)
    ccol, crow = _forget_cumsum(lg.reshape(b, t, LANES), bias)
    fox = _fox_attn(qkvf.reshape(b, t, 3 * WIDTH), ccol, crow, tq=256)
    dil = _dil_attn(qkvd.reshape(b, t, 3 * WIDTH), tq=256)
    h1, xn2 = _out_proj(fox.reshape(n, WIDTH), dil.reshape(n, WIDTH), x2,
                        w_out[0].astype(BF16), norm2_g[0].reshape(1, d), tm=512)

    wq_t = peer_wq[0].T.astype(BF16)
    keys = peer_subkeys[0].reshape(2 * PEER_HEADS, PEER_NKEYS, PEER_HALF).astype(BF16)
    s1, s2, e1, e2, tau = _peer_route(xn2, wq_t, keys, tm=256)
    out = _peer_dense(xn2, peer_u[0].astype(BF16), peer_v[0].T.astype(BF16),
                      s1, s2, e1, e2, tau, h1, normf_g.reshape(1, d), tm=512, te=1024)
    return out.reshape(b, t, d)
```
